```python
import jax, jax.numpy as jnp
from jax import lax
import numpy as np

D_MODEL = 2048
BATCH = 4
SEQ = 8192
DEPTH = 1
DEC_BATCH = 8
DEC_SEQ = 16
PAST_LEN = 4096

CHUNK = 64
GMLP_CHUNK = 128
GMLP_GROUPS = 8
GMLP_DG = 128
GMLP_W = GMLP_GROUPS * GMLP_DG
FOX_HEADS = 8
FOX_HD = 128
FOX_W = FOX_HEADS * FOX_HD
FOX_SCALE = FOX_HD ** -0.5
Q_BLOCK = 128
MOE_GROUPS = 4
EXPERTS_PER_GROUP = 8
N_EXPERTS = MOE_GROUPS * EXPERTS_PER_GROUP
TOP_K = 2
D_EXPERT = 512
MOE_BLOCK = 128
ALPHA = (2 * DEPTH) ** 0.25
BETA = (8 * DEPTH) ** -0.25
LN_EPS = 1e-5
NEG_INF = -1e30

OFF_U = 0
OFF_V = OFF_U + GMLP_W
OFF_Q = OFF_V + GMLP_W
OFF_K = OFF_Q + FOX_W
OFF_FV = OFF_K + FOX_W
OFF_F = OFF_FV + FOX_W
OFF_GA = OFF_F + FOX_HEADS
OFF_GB = OFF_GA + D_MODEL
D_IN = OFF_GB + D_MODEL

kernel_name = "hybrid_gmlp_fox_hmoe_stream_step"


def layer_norm(x, g, b):
    xf = x.astype(jnp.float32)
    mu = jnp.mean(xf, axis=-1, keepdims=True)
    xc = xf - mu
    var = jnp.mean(xc * xc, axis=-1, keepdims=True)
    return (xc * lax.rsqrt(var + LN_EPS) * g.astype(jnp.float32) + b.astype(jnp.float32)).astype(x.dtype)


def in_proj(x, w_in, b_in):
    z = x @ w_in + b_in
    u = jax.nn.gelu(z[..., OFF_U:OFF_V], approximate=False)
    v = jax.nn.gelu(z[..., OFF_V:OFF_Q], approximate=False)
    q = z[..., OFF_Q:OFF_K]
    k = z[..., OFF_K:OFF_FV]
    fv = z[..., OFF_FV:OFF_F]
    logf = jax.nn.log_sigmoid(z[..., OFF_F:OFF_GA].astype(jnp.float32))
    ga = z[..., OFF_GA:OFF_GB]
    gb = z[..., OFF_GB:]
    return u, v, q, k, fv, logf, ga, gb


def gmlp_branch(u, v, ln_g, ln_b, ws, bs):
    B, T, _ = u.shape
    vn = layer_norm(v, ln_g, ln_b)
    L = min(T, GMLP_CHUNK)
    C = T // L
    v5 = vn.reshape(B, C, L, GMLP_GROUPS, GMLP_DG)
    tril = jnp.tril(jnp.ones((L, L), dtype=bool))
    wsm = jnp.where(tril[None], ws[:, :L, :L], jnp.zeros((), ws.dtype))
    mix = jnp.einsum('gts,bcsgd->bctgd', wsm, v5) + bs[:, :L].T[None, None, :, :, None]
    y = u * mix.reshape(B, T, GMLP_W)
    return y, vn.reshape(B, T, GMLP_GROUPS, GMLP_DG)


def fox_prompt(q, k, v, logf):
    B, S, H, Dh = q.shape
    c = jnp.cumsum(logf, axis=1).transpose(0, 2, 1)
    kpos = jnp.arange(S)

    def q_block(i):
        s0 = i * Q_BLOCK
        qb = lax.dynamic_slice_in_dim(q, s0, Q_BLOCK, axis=1)
        cq = lax.dynamic_slice_in_dim(c, s0, Q_BLOCK, axis=2)
        sc = jnp.einsum('bqhd,bkhd->bhqk', qb, k, preferred_element_type=jnp.float32) * FOX_SCALE
        sc = sc + cq[..., :, None] - c[:, :, None, :]
        qpos = s0 + jnp.arange(Q_BLOCK)
        sc = jnp.where(kpos[None, :] <= qpos[:, None], sc, NEG_INF)
        p = jax.nn.softmax(sc, axis=-1).astype(v.dtype)
        return jnp.einsum('bhqk,bkhd->bqhd', p, v)

    out = lax.map(q_block, jnp.arange(S // Q_BLOCK))
    return out.transpose(1, 0, 2, 3, 4).reshape(B, S, H * Dh)


def fox_sample(q, k, v, logf, k_cache, v_cache, logf_cache):
    B, T, H, Dh = q.shape
    P = k_cache.shape[1]
    lc = logf_cache.astype(jnp.float32)
    suffix = lax.cumsum(lc, axis=1, reverse=True) - lc
    c_new = jnp.cumsum(logf, axis=1)
    key_c = jnp.concatenate([-suffix, c_new], axis=1).transpose(0, 2, 1)
    k_all = jnp.concatenate([k_cache.astype(k.dtype), k], axis=1)
    v_all = jnp.concatenate([v_cache.astype(v.dtype), v], axis=1)
    sc = jnp.einsum('bqhd,bkhd->bhqk', q, k_all, preferred_element_type=jnp.float32) * FOX_SCALE
    sc = sc + c_new.transpose(0, 2, 1)[..., :, None] - key_c[:, :, None, :]
    kidx = jnp.arange(P + T)
    qidx = P + jnp.arange(T)
    sc = jnp.where(kidx[None, :] <= qidx[:, None], sc, NEG_INF)
    p = jax.nn.softmax(sc, axis=-1).astype(v.dtype)
    return jnp.einsum('bhqk,bkhd->bqhd', p, v_all).reshape(B, T, H * Dh)


def merge_branches(y_a, y_b, ga, gb, w_a, w_b, w_o):
    m = jax.nn.sigmoid(ga) * (y_a @ w_a) + jax.nn.sigmoid(gb) * (y_b @ w_b)
    return m @ w_o


def grouped_experts(x2d, eidx, ew, w_gate, w_up, w_down):
    N, D = x2d.shape
    A = N * TOP_K
    e_flat = eidx.reshape(A)
    tok = jnp.repeat(jnp.arange(N, dtype=jnp.int32), TOP_K)
    wt = ew.reshape(A)
    order = jnp.argsort(e_flat)
    e_s = e_flat[order]
    counts = jnp.bincount(e_flat, length=N_EXPERTS)
    pcounts = (counts + MOE_BLOCK - 1) // MOE_BLOCK * MOE_BLOCK
    start = jnp.cumsum(counts) - counts
    pend = jnp.cumsum(pcounts)
    pstart = pend - pcounts
    dest = pstart[e_s] + (jnp.arange(A) - start[e_s])
    nb = -(-A // MOE_BLOCK) + N_EXPERTS
    L = nb * MOE_BLOCK
    row_tok = jnp.full((L,), N, dtype=jnp.int32).at[dest].set(tok[order])
    row_w = jnp.zeros((L,), jnp.float32).at[dest].set(wt[order])
    blk_e = jnp.minimum(jnp.searchsorted(pend, jnp.arange(nb) * MOE_BLOCK, side='right'), N_EXPERTS - 1)
    x_pad = jnp.concatenate([x2d, jnp.zeros((1, D), x2d.dtype)], axis=0)

    def expert_block(args):
        rt, rw, e = args
        xb = x_pad[rt]
        h = jax.nn.silu(xb @ w_gate[e]) * (xb @ w_up[e])
        return (h @ w_down[e]) * rw[:, None].astype(xb.dtype)

    yb = lax.map(expert_block, (row_tok.reshape(nb, MOE_BLOCK), row_w.reshape(nb, MOE_BLOCK), blk_e))
    y = jax.ops.segment_sum(yb.reshape(L, D), row_tok, num_segments=N + 1)
    return y[:N]


def hier_moe(x2d, w_rg, b_rg, w_re, b_re, w_gate, w_up, w_down):
    N = x2d.shape[0]
    lg = (x2d @ w_rg + b_rg).astype(jnp.float32)
    pg = jax.nn.softmax(lg, axis=-1)
    pg_top, g_top = lax.top_k(pg, 1)
    le = (x2d @ w_re + b_re).astype(jnp.float32).reshape(N, MOE_GROUPS, EXPERTS_PER_GROUP)
    le_g = jnp.take_along_axis(le, g_top[:, :, None], axis=1)[:, 0]
    pe = jax.nn.softmax(le_g, axis=-1)
    pe_top, e_top = lax.top_k(pe, TOP_K)
    pe_top = pe_top / jnp.sum(pe_top, axis=-1, keepdims=True)
    weights = pg_top * pe_top
    eidx = g_top * EXPERTS_PER_GROUP + e_top
    return grouped_experts(x2d, eidx, weights, w_gate, w_up, w_down)


def post_norm_and_moe(x, mix_out, ln1_g, ln1_b, w_rg, b_rg, w_re, b_re, w_gate, w_up, w_down, ln2_g, ln2_b):
    h = layer_norm(ALPHA * x + mix_out, ln1_g, ln1_b)
    B, T, D = h.shape
    f = hier_moe(h.reshape(B * T, D), w_rg, b_rg, w_re, b_re, w_gate, w_up, w_down).reshape(B, T, D)
    return layer_norm(ALPHA * h + f, ln2_g, ln2_b)


def setup_inputs(seed: int = 0) -> dict:
    key = jax.random.key(seed)
    ks = jax.random.split(key, 26)
    D = D_MODEL
    nrm = jax.random.normal
    x_prompt = nrm(ks[0], (BATCH, SEQ, D), jnp.float32)
    x_sample = nrm(ks[1], (DEC_BATCH, DEC_SEQ, D), jnp.float32)
    cache_fox_k = nrm(ks[2], (DEPTH, DEC_BATCH, PAST_LEN, FOX_HEADS, FOX_HD), jnp.float32)
    cache_fox_v = nrm(ks[3], (DEPTH, DEC_BATCH, PAST_LEN, FOX_HEADS, FOX_HD), jnp.float32) * BETA
    cache_fox_logf = jax.nn.log_sigmoid(4.0 + nrm(ks[4], (DEPTH, DEC_BATCH, PAST_LEN, FOX_HEADS), jnp.float32))
    col_scale = jnp.ones((D_IN,), jnp.float32).at[OFF_FV:OFF_F].set(BETA).at[OFF_F:OFF_GA].set(0.1)
    w_in = nrm(ks[5], (DEPTH, D, D_IN), jnp.float32) * (D ** -0.5) * col_scale
    f_bias = jax.random.uniform(ks[6], (DEPTH, FOX_HEADS), jnp.float32, 2.0, 6.0)
    b_in = (0.02 * nrm(ks[7], (DEPTH, D_IN), jnp.float32)).at[:, OFF_F:OFF_GA].set(f_bias)
    gmlp_ln_g = 1.0 + 0.1 * nrm(ks[8], (DEPTH, GMLP_W), jnp.float32)
    gmlp_ln_b = 0.02 * nrm(ks[9], (DEPTH, GMLP_W), jnp.float32)
    gmlp_ws = nrm(ks[10], (DEPTH, GMLP_GROUPS, GMLP_CHUNK, GMLP_CHUNK), jnp.float32) * (GMLP_CHUNK ** -0.5)
    gmlp_bs = 1.0 + 0.1 * nrm(ks[11], (DEPTH, GMLP_GROUPS, GMLP_CHUNK), jnp.float32)
    w_a = nrm(ks[12], (DEPTH, GMLP_W, D), jnp.float32) * (GMLP_W ** -0.5) * BETA
    w_b = nrm(ks[13], (DEPTH, FOX_W, D), jnp.float32) * (FOX_W ** -0.5) * BETA
    w_o = nrm(ks[14], (DEPTH, D, D), jnp.float32) * (D ** -0.5) * BETA
    ln1_g = 1.0 + 0.1 * nrm(ks[15], (DEPTH, D), jnp.float32)
    ln1_b = 0.02 * nrm(ks[16], (DEPTH, D), jnp.float32)
    w_rg = nrm(ks[17], (DEPTH, D, MOE_GROUPS), jnp.float32) * (D ** -0.5)
    b_rg = 0.01 * nrm(ks[18], (DEPTH, MOE_GROUPS), jnp.float32)
    w_re = nrm(ks[19], (DEPTH, D, N_EXPERTS), jnp.float32) * (D ** -0.5)
    b_re = 0.01 * nrm(ks[20], (DEPTH, N_EXPERTS), jnp.float32)
    w_gate = nrm(ks[21], (DEPTH, N_EXPERTS, D, D_EXPERT), jnp.float32) * (D ** -0.5) * BETA
    w_up = nrm(ks[22], (DEPTH, N_EXPERTS, D, D_EXPERT), jnp.float32) * (D ** -0.5) * BETA
    w_down = nrm(ks[23], (DEPTH, N_EXPERTS, D_EXPERT, D), jnp.float32) * (D_EXPERT ** -0.5) * BETA
    ln2_g = 1.0 + 0.1 * nrm(ks[24], (DEPTH, D), jnp.float32)
    ln2_b = 0.02 * nrm(ks[25], (DEPTH, D), jnp.float32)
    return {"x_prompt": x_prompt, "x_sample": x_sample,
            "cache_fox_k": cache_fox_k, "cache_fox_v": cache_fox_v, "cache_fox_logf": cache_fox_logf,
            "w_in": w_in, "b_in": b_in, "gmlp_ln_g": gmlp_ln_g, "gmlp_ln_b": gmlp_ln_b,
            "gmlp_ws": gmlp_ws, "gmlp_bs": gmlp_bs, "w_a": w_a, "w_b": w_b, "w_o": w_o,
            "ln1_g": ln1_g, "ln1_b": ln1_b, "w_rg": w_rg, "b_rg": b_rg, "w_re": w_re, "b_re": b_re,
            "w_gate": w_gate, "w_up": w_up, "w_down": w_down, "ln2_g": ln2_g, "ln2_b": ln2_b}


def reference(x_prompt, x_sample, cache_fox_k, cache_fox_v, cache_fox_logf,
              w_in, b_in, gmlp_ln_g, gmlp_ln_b, gmlp_ws, gmlp_bs, w_a, w_b, w_o,
              ln1_g, ln1_b, w_rg, b_rg, w_re, b_re, w_gate, w_up, w_down, ln2_g, ln2_b):
    yp = x_prompt
    ys = x_sample
    kp_l, vp_l, fp_l, ks_l, vs_l, fs_l, gv_l = [], [], [], [], [], [], []
    for l in range(DEPTH):
        B, S, _ = yp.shape
        u, v, q, k, fv, logf, ga, gb = in_proj(yp, w_in[l], b_in[l])
        y_a, _ = gmlp_branch(u, v, gmlp_ln_g[l], gmlp_ln_b[l], gmlp_ws[l], gmlp_bs[l])
        k4 = k.reshape(B, S, FOX_HEADS, FOX_HD)
        v4 = fv.reshape(B, S, FOX_HEADS, FOX_HD)
        y_b = fox_prompt(q.reshape(B, S, FOX_HEADS, FOX_HD), k4, v4, logf)
        mix = merge_branches(y_a, y_b, ga, gb, w_a[l], w_b[l], w_o[l])
        kp_l.append(k4)
        vp_l.append(v4)
        fp_l.append(logf)
        yp = post_norm_and_moe(yp, mix, ln1_g[l], ln1_b[l], w_rg[l], b_rg[l], w_re[l], b_re[l],
                               w_gate[l], w_up[l], w_down[l], ln2_g[l], ln2_b[l])
        Bs, T, _ = ys.shape
        u, v, q, k, fv, logf, ga, gb = in_proj(ys, w_in[l], b_in[l])
        y_a, v_rows = gmlp_branch(u, v, gmlp_ln_g[l], gmlp_ln_b[l], gmlp_ws[l], gmlp_bs[l])
        k4 = k.reshape(Bs, T, FOX_HEADS, FOX_HD)
        v4 = fv.reshape(Bs, T, FOX_HEADS, FOX_HD)
        y_b = fox_sample(q.reshape(Bs, T, FOX_HEADS, FOX_HD), k4, v4, logf,
                         cache_fox_k[l], cache_fox_v[l], cache_fox_logf[l])
        mix = merge_branches(y_a, y_b, ga, gb, w_a[l], w_b[l], w_o[l])
        ks_l.append(k4)
        vs_l.append(v4)
        fs_l.append(logf)
        gv_l.append(v_rows)
        ys = post_norm_and_moe(ys, mix, ln1_g[l], ln1_b[l], w_rg[l], b_rg[l], w_re[l], b_re[l],
                               w_gate[l], w_up[l], w_down[l], ln2_g[l], ln2_b[l])
    return (yp, ys, jnp.stack(kp_l), jnp.stack(vp_l), jnp.stack(fp_l),
            jnp.stack(ks_l), jnp.stack(vs_l), jnp.stack(fs_l), jnp.stack(gv_l))
```

```python
import functools

import jax
import jax.numpy as jnp
from jax import lax
from jax.experimental import pallas as pl
from jax.experimental.pallas import tpu as pltpu

D_MODEL = 2048
GMLP_CHUNK = 128
GMLP_GROUPS = 8
GMLP_DG = 128
GMLP_W = GMLP_GROUPS * GMLP_DG
FOX_HEADS = 8
FOX_HD = 128
FOX_W = FOX_HEADS * FOX_HD
FOX_SCALE = FOX_HD ** -0.5
MOE_GROUPS = 4
EXPERTS_PER_GROUP = 8
N_EXPERTS = MOE_GROUPS * EXPERTS_PER_GROUP
TOP_K = 2
D_EXPERT = 512
DEPTH = 1
ALPHA = (2 * DEPTH) ** 0.25
LN_EPS = 1e-5
NEG_INF = -1e30

OFF_U = 0
OFF_V = OFF_U + GMLP_W
OFF_Q = OFF_V + GMLP_W
OFF_K = OFF_Q + FOX_W
OFF_FV = OFF_K + FOX_W
OFF_F = OFF_FV + FOX_W
OFF_GA = OFF_F + FOX_HEADS
OFF_GB = OFF_GA + D_MODEL
D_IN = OFF_GB + D_MODEL

LANES = 128
EXPERT_LANE0 = 8
MOE_ROWS = 256
VMEM_LIMIT = 56 * 1024 * 1024

F32 = jnp.float32
BF16 = jnp.bfloat16


def _cparams(sem):
    return pltpu.CompilerParams(dimension_semantics=sem, vmem_limit_bytes=VMEM_LIMIT)


def _cast_x(x_ref, xb_ref):
    @pl.when(pl.program_id(1) == 0)
    def _():
        xb_ref[...] = x_ref[...].astype(BF16)


def _proj_act_kernel(act, x_ref, w_ref, b_ref, o_ref, xb_ref):
    _cast_x(x_ref, xb_ref)
    z = jnp.dot(xb_ref[...], w_ref[...], preferred_element_type=F32) + b_ref[...]
    if act == "gelu":
        z = 0.5 * z * (1.0 + lax.erf(z * (2.0 ** -0.5)))
    else:
        z = jax.nn.sigmoid(z)
    o_ref[...] = z.astype(o_ref.dtype)


def _proj_act(x, w, b, act, tm, tn):
    n, d = x.shape
    nc = w.shape[1]
    return pl.pallas_call(
        functools.partial(_proj_act_kernel, act),
        grid=(n // tm, nc // tn),
        in_specs=[pl.BlockSpec((tm, d), lambda i, j: (i, 0)),
                  pl.BlockSpec((d, tn), lambda i, j: (0, j)),
                  pl.BlockSpec((1, tn), lambda i, j: (0, j))],
        out_specs=pl.BlockSpec((tm, tn), lambda i, j: (i, j)),
        out_shape=jax.ShapeDtypeStruct((n, nc), BF16),
        scratch_shapes=[pltpu.VMEM((tm, d), BF16)],
        compiler_params=_cparams(("arbitrary", "arbitrary")),
        name="proj_" + act,
    )(x, w, b)


def _log_sigmoid(z):
    return jnp.minimum(z, 0.0) - jnp.log1p(jnp.exp(-jnp.abs(z)))


def _proj_qkv_kernel(x_ref, w_ref, b_ref, wf_ref, bf_ref,
                     q_ref, k32_ref, kb_ref, v32_ref, vb_ref, lf_ref, xb_ref):
    j = pl.program_id(1)
    _cast_x(x_ref, xb_ref)
    z = jnp.dot(xb_ref[...], w_ref[...], preferred_element_type=F32) + b_ref[...]

    @pl.when(j == 0)
    def _():
        q_ref[...] = (z * FOX_SCALE).astype(BF16)
        zf = jnp.dot(xb_ref[...], wf_ref[...], preferred_element_type=F32) + bf_ref[...]
        lf_ref[...] = _log_sigmoid(zf)

    @pl.when(j == 1)
    def _():
        k32_ref[...] = z
        kb_ref[...] = z.astype(BF16)

    @pl.when(j == 2)
    def _():
        v32_ref[...] = z
        vb_ref[...] = z.astype(BF16)


def _proj_qkv(x, w, b, wf, bf, tm):
    n, d = x.shape
    row = lambda i, j: (i, 0)
    o_spec = pl.BlockSpec((tm, FOX_W), row)
    return pl.pallas_call(
        _proj_qkv_kernel,
        grid=(n // tm, 3),
        in_specs=[pl.BlockSpec((tm, d), row),
                  pl.BlockSpec((d, FOX_W), lambda i, j: (0, j)),
                  pl.BlockSpec((1, FOX_W), lambda i, j: (0, j)),
                  pl.BlockSpec((d, LANES), lambda i, j: (0, 0)),
                  pl.BlockSpec((1, LANES), lambda i, j: (0, 0))],
        out_specs=[o_spec, o_spec, o_spec, o_spec, o_spec, pl.BlockSpec((tm, LANES), row)],
        out_shape=[jax.ShapeDtypeStruct((n, FOX_W), BF16),
                   jax.ShapeDtypeStruct((n, FOX_W), F32),
                   jax.ShapeDtypeStruct((n, FOX_W), BF16),
                   jax.ShapeDtypeStruct((n, FOX_W), F32),
                   jax.ShapeDtypeStruct((n, FOX_W), BF16),
                   jax.ShapeDtypeStruct((n, LANES), F32)],
        scratch_shapes=[pltpu.VMEM((tm, d), BF16)],
        compiler_params=_cparams(("arbitrary", "arbitrary")),
        name="proj_qkv",
    )(x, w, b, wf, bf)


CUM_TILE = 256


def _split3(x):
    x1 = x.astype(BF16)
    r = x - x1.astype(F32)
    x2 = r.astype(BF16)
    x3 = (r - x2.astype(F32)).astype(BF16)
    return x1, x2, x3


def _cumsum_kernel(nt, x_ref, o_ref):
    ri = lax.broadcasted_iota(jnp.int32, (CUM_TILE, CUM_TILE), 0)
    ci = lax.broadcasted_iota(jnp.int32, (CUM_TILE, CUM_TILE), 1)
    upper = jnp.where(ri <= ci, 1.0, 0.0).astype(BF16)
    carry = jnp.zeros((FOX_HEADS, 1), F32)
    for t in range(nt):
        x = x_ref[:, t * CUM_TILE:(t + 1) * CUM_TILE]
        x1, x2, x3 = _split3(x)
        c = (jnp.dot(x3, upper, preferred_element_type=F32)
             + jnp.dot(x2, upper, preferred_element_type=F32)
             + jnp.dot(x1, upper, preferred_element_type=F32)) + carry
        o_ref[:, t * CUM_TILE:(t + 1) * CUM_TILE] = c
        carry = c[:, CUM_TILE - 1:CUM_TILE]


def _cumsum_lanes(xt):
    b, h, s = xt.shape
    nt = s // CUM_TILE
    spec = pl.BlockSpec((None, h, s), lambda i: (i, 0, 0))
    return pl.pallas_call(
        functools.partial(_cumsum_kernel, nt),
        grid=(b,),
        in_specs=[spec],
        out_specs=spec,
        out_shape=jax.ShapeDtypeStruct((b, h, s), F32),
        compiler_params=_cparams(("arbitrary",)),
        name="cumsum_logf",
    )(xt)


def _flash_update(s, v, m_ref, l_ref, acc_ref, h):
    tk = s.shape[1]
    hs = slice(h * FOX_HD, (h + 1) * FOX_HD)
    m_prev = m_ref[:, hs]
    l_prev = l_ref[:, hs]
    m_cur = jnp.max(s, axis=1, keepdims=True)
    m_new = jnp.maximum(m_prev, m_cur)
    alpha = jnp.exp(m_prev - m_new)
    p = jnp.exp(s - pltpu.repeat(m_new, tk // LANES, axis=1))
    l_ref[:, hs] = alpha * l_prev + jnp.sum(p, axis=1, keepdims=True)
    m_ref[:, hs] = m_new
    acc_ref[:, hs] = alpha * acc_ref[:, hs] + jnp.dot(p.astype(BF16), v, preferred_element_type=F32)


def _fox_prompt_kernel(qi_ref, kj_ref, q_ref, k_ref, v_ref, ck_ref, o_ref, m_ref, l_ref, acc_ref):
    step = pl.program_id(1)
    qi = qi_ref[step]
    kj = kj_ref[step]
    tq = q_ref.shape[0]
    tk = k_ref.shape[0]

    @pl.when(kj == 0)
    def _():
        m_ref[...] = jnp.full(m_ref.shape, NEG_INF, F32)
        l_ref[...] = jnp.zeros(l_ref.shape, F32)
        acc_ref[...] = jnp.zeros(acc_ref.shape, F32)

    def body(masked):
        if masked:
            row = lax.broadcasted_iota(jnp.int32, (tq, tk), 0)
            col = lax.broadcasted_iota(jnp.int32, (tq, tk), 1)
            keep = col <= row
        for h in range(FOX_HEADS):
            hs = slice(h * FOX_HD, (h + 1) * FOX_HD)
            s = lax.dot_general(q_ref[:, hs], k_ref[:, hs], (((1,), (1,)), ((), ())),
                                preferred_element_type=F32)
            s = s - ck_ref[h:h + 1, :]
            if masked:
                s = jnp.where(keep, s, NEG_INF)
            _flash_update(s, v_ref[:, hs], m_ref, l_ref, acc_ref, h)

    @pl.when(kj < qi)
    def _():
        body(False)

    @pl.when(kj == qi)
    def _():
        body(True)
        o_ref[...] = (acc_ref[...] / l_ref[...]).astype(o_ref.dtype)


def _fox_prompt(q, k, v, ct, tq):
    b, s, w = q.shape
    nq = s // tq
    qi_l, kj_l = [], []
    for i in range(nq):
        for j in range(i + 1):
            qi_l.append(i)
            kj_l.append(j)
    qi_a = jnp.asarray(qi_l, jnp.int32)
    kj_a = jnp.asarray(kj_l, jnp.int32)
    grid_spec = pltpu.PrefetchScalarGridSpec(
        num_scalar_prefetch=2,
        grid=(b, len(qi_l)),
        in_specs=[pl.BlockSpec((None, tq, w), lambda bi, st, qi, kj: (bi, qi[st], 0)),
                  pl.BlockSpec((None, tq, w), lambda bi, st, qi, kj: (bi, kj[st], 0)),
                  pl.BlockSpec((None, tq, w), lambda bi, st, qi, kj: (bi, kj[st], 0)),
                  pl.BlockSpec((None, FOX_HEADS, tq), lambda bi, st, qi, kj: (bi, 0, kj[st]))],
        out_specs=pl.BlockSpec((None, tq, w), lambda bi, st, qi, kj: (bi, qi[st], 0)),
        scratch_shapes=[pltpu.VMEM((tq, w), F32), pltpu.VMEM((tq, w), F32), pltpu.VMEM((tq, w), F32)],
    )
    return pl.pallas_call(
        _fox_prompt_kernel,
        grid_spec=grid_spec,
        out_shape=jax.ShapeDtypeStruct((b, s, w), BF16),
        compiler_params=_cparams(("arbitrary", "arbitrary")),
        name="fox_prompt",
    )(qi_a, kj_a, q, k, v, ct)


def _fox_sample_kernel(q_ref, kc_ref, vc_ref, cc_ref, kn_ref, vn_ref, cn_ref, o_ref, m_ref, l_ref, acc_ref):
    c = pl.program_id(1)
    t = q_ref.shape[0]

    @pl.when(c == 0)
    def _():
        m_ref[...] = jnp.full(m_ref.shape, NEG_INF, F32)
        l_ref[...] = jnp.zeros(l_ref.shape, F32)
        acc_ref[...] = jnp.zeros(acc_ref.shape, F32)

    nt = (((1,), (1,)), ((), ()))
    for h in range(FOX_HEADS):
        hs = slice(h * FOX_HD, (h + 1) * FOX_HD)
        s = lax.dot_general(q_ref[:, hs], kc_ref[:, hs].astype(BF16), nt, preferred_element_type=F32)
        s = s - cc_ref[h:h + 1, :]
        _flash_update(s, vc_ref[:, hs].astype(BF16), m_ref, l_ref, acc_ref, h)

    @pl.when(c == pl.num_programs(1) - 1)
    def _():
        row = lax.broadcasted_iota(jnp.int32, (t, t), 0)
        col = lax.broadcasted_iota(jnp.int32, (t, t), 1)
        keep = col <= row
        for h in range(FOX_HEADS):
            hs = slice(h * FOX_HD, (h + 1) * FOX_HD)
            s = lax.dot_general(q_ref[:, hs], kn_ref[:, hs], nt, preferred_element_type=F32)
            s = jnp.where(keep, s - cn_ref[h:h + 1, :], NEG_INF)
            m_prev = m_ref[:, hs]
            m_new = jnp.maximum(m_prev, jnp.max(s, axis=1, keepdims=True))
            alpha = jnp.exp(m_prev - m_new)
            p = jnp.exp(s - m_new[:, :t])
            l_new = alpha * l_ref[:, hs] + jnp.sum(p, axis=1, keepdims=True)
            acc = alpha * acc_ref[:, hs] + jnp.dot(p.astype(BF16), vn_ref[:, hs], preferred_element_type=F32)
            o_ref[:, hs] = (acc / l_new).astype(o_ref.dtype)


def _fox_sample(q, kc, vc, cc, kn, vn, cn, tk):
    b, t, w = q.shape
    p = kc.shape[1]
    new_spec = pl.BlockSpec((None, t, w), lambda bi, c: (bi, 0, 0))
    cache_spec = pl.BlockSpec((None, tk, w), lambda bi, c: (bi, c, 0))
    return pl.pallas_call(
        _fox_sample_kernel,
        grid=(b, p // tk),
        in_specs=[new_spec, cache_spec, cache_spec,
                  pl.BlockSpec((None, FOX_HEADS, tk), lambda bi, c: (bi, 0, c)),
                  new_spec, new_spec,
                  pl.BlockSpec((None, FOX_HEADS, t), lambda bi, c: (bi, 0, 0))],
        out_specs=new_spec,
        out_shape=jax.ShapeDtypeStruct((b, t, w), BF16),
        scratch_shapes=[pltpu.VMEM((t, w), F32), pltpu.VMEM((t, w), F32), pltpu.VMEM((t, w), F32)],
        compiler_params=_cparams(("arbitrary", "arbitrary")),
        name="fox_sample",
    )(q, kc, vc, cc, kn, vn, cn)


def _layer_norm(x, g, b):
    mu = jnp.mean(x, axis=-1, keepdims=True)
    xc = x - mu
    var = jnp.mean(xc * xc, axis=-1, keepdims=True)
    return xc * lax.rsqrt(var + LN_EPS) * g + b


def _lane_pick(vals, lane, idx):
    return jnp.sum(jnp.where(lane == idx, vals, 0.0), axis=1, keepdims=True)


def _merge_kernel(emit_vn, x_ref, uv_ref, yb_ref, sg_ref, lng_ref, lnb_ref, ws_ref, bs_ref,
                  wa_ref, wb_ref, wo_ref, g1_ref, b1_ref, wrh_ref, wrl_ref, br_ref, *refs):
    if emit_vn:
        h32_ref, hb_ref, rt_ref, cnt_ref, vn_ref, ya_ref, run_ref = refs
    else:
        h32_ref, hb_ref, rt_ref, cnt_ref, ya_ref, run_ref = refs
    tm = x_ref.shape[0]

    @pl.when(pl.program_id(0) == 0)
    def _():
        run_ref[...] = jnp.zeros(run_ref.shape, F32)

    vn = _layer_norm(uv_ref[:, GMLP_W:].astype(F32), lng_ref[...], lnb_ref[...])
    if emit_vn:
        vn_ref[...] = vn
    vnb = vn.astype(BF16)
    for c in range(tm // GMLP_CHUNK):
        rs = slice(c * GMLP_CHUNK, (c + 1) * GMLP_CHUNK)
        for g in range(GMLP_GROUPS):
            gs = slice(g * GMLP_DG, (g + 1) * GMLP_DG)
            mix = jnp.dot(ws_ref[g], vnb[rs, gs], preferred_element_type=F32) + bs_ref[g]
            ya_ref[rs, gs] = (uv_ref[rs, gs].astype(F32) * mix).astype(BF16)

    a = jnp.dot(ya_ref[...], wa_ref[...], preferred_element_type=F32)
    bm = jnp.dot(yb_ref[...], wb_ref[...], preferred_element_type=F32)
    m = sg_ref[:, :D_MODEL].astype(F32) * a + sg_ref[:, D_MODEL:].astype(F32) * bm
    mix_out = jnp.dot(m.astype(BF16), wo_ref[...], preferred_element_type=F32)
    h = _layer_norm(ALPHA * x_ref[...] + mix_out, g1_ref[...], b1_ref[...])
    h32_ref[...] = h
    hb = h.astype(BF16)
    hb_ref[...] = hb

    hl = (h - hb.astype(F32)).astype(BF16)
    logits = (jnp.dot(hl, wrh_ref[...], preferred_element_type=F32)
              + jnp.dot(hb, wrl_ref[...], preferred_element_type=F32)
              + jnp.dot(hb, wrh_ref[...], preferred_element_type=F32)) + br_ref[...]

    lane = lax.broadcasted_iota(jnp.int32, (tm, LANES), 1)
    lanef = lane.astype(F32)
    big = float(LANES)
    lg = jnp.where(lane < MOE_GROUPS, logits, NEG_INF)
    gmax = jnp.max(lg, axis=1, keepdims=True)
    g_top = jnp.min(jnp.where(lg == gmax, lanef, big), axis=1, keepdims=True)
    pg_top = 1.0 / jnp.sum(jnp.exp(lg - gmax), axis=1, keepdims=True)
    e_lo = EXPERT_LANE0 + g_top * EXPERTS_PER_GROUP
    in_grp = (lanef >= e_lo) & (lanef < e_lo + EXPERTS_PER_GROUP)
    le = jnp.where(in_grp, logits, NEG_INF)
    l1 = jnp.max(le, axis=1, keepdims=True)
    i1 = jnp.min(jnp.where(le == l1, lanef, big), axis=1, keepdims=True)
    le2 = jnp.where(lanef == i1, NEG_INF, le)
    l2 = jnp.max(le2, axis=1, keepdims=True)
    i2 = jnp.min(jnp.where(le2 == l2, lanef, big), axis=1, keepdims=True)
    e2x = jnp.exp(l2 - l1)
    w1 = pg_top / (1.0 + e2x)
    w2 = pg_top * e2x / (1.0 + e2x)

    onehot = jnp.where((lanef == i1) | (lanef == i2), 1.0, 0.0)
    ri = lax.broadcasted_iota(jnp.int32, (tm, tm), 0)
    ci = lax.broadcasted_iota(jnp.int32, (tm, tm), 1)
    lower = jnp.where(ci < ri, 1.0, 0.0).astype(BF16)
    before = jnp.dot(lower, onehot.astype(BF16), preferred_element_type=F32) + run_ref[...]
    r1 = _lane_pick(before, lanef, i1)
    r2 = _lane_pick(before, lanef, i2)
    run_ref[...] = run_ref[...] + jnp.sum(onehot, axis=0, keepdims=True)
    cnt_ref[...] = run_ref[...]

    out = jnp.where(lane == 0, i1 - EXPERT_LANE0, 0.0)
    out = jnp.where(lane == 1, i2 - EXPERT_LANE0, out)
    out = jnp.where(lane == 2, w1, out)
    out = jnp.where(lane == 3, w2, out)
    out = jnp.where(lane == 4, r1, out)
    out = jnp.where(lane == 5, r2, out)
    rt_ref[...] = out


def _merge(x, uv, yb, sg, lng, lnb, ws, bs, wa, wb, wo, g1, b1, wrh, wrl, br, tm, emit_vn):
    n, d = x.shape
    row = lambda i: (i, 0)
    fixed2 = lambda i: (0, 0)
    fixed3 = lambda i: (0, 0, 0)

    def const(shape):
        return pl.BlockSpec(shape, fixed2 if len(shape) == 2 else fixed3, pipeline_mode=pl.Buffered(1))

    in_specs = [pl.BlockSpec((tm, d), row), pl.BlockSpec((tm, 2 * GMLP_W), row),
                pl.BlockSpec((tm, FOX_W), row), pl.BlockSpec((tm, 2 * d), row),
                const((1, GMLP_W)), const((1, GMLP_W)),
                const((GMLP_GROUPS, GMLP_CHUNK, GMLP_CHUNK)), const((GMLP_GROUPS, GMLP_CHUNK, GMLP_DG)),
                const((GMLP_W, d)), const((FOX_W, d)), const((d, d)),
                const((1, d)), const((1, d)),
                const((d, LANES)), const((d, LANES)), const((1, LANES))]
    out_specs = [pl.BlockSpec((tm, d), row), pl.BlockSpec((tm, d), row),
                 pl.BlockSpec((tm, LANES), row), pl.BlockSpec((1, LANES), fixed2)]
    out_shape = [jax.ShapeDtypeStruct((n, d), F32), jax.ShapeDtypeStruct((n, d), BF16),
                 jax.ShapeDtypeStruct((n, LANES), F32), jax.ShapeDtypeStruct((1, LANES), F32)]
    if emit_vn:
        out_specs.append(pl.BlockSpec((tm, GMLP_W), row))
        out_shape.append(jax.ShapeDtypeStruct((n, GMLP_W), F32))
    return pl.pallas_call(
        functools.partial(_merge_kernel, emit_vn),
        grid=(n // tm,),
        in_specs=in_specs,
        out_specs=out_specs,
        out_shape=out_shape,
        scratch_shapes=[pltpu.VMEM((tm, GMLP_W), BF16), pltpu.VMEM((1, LANES), F32)],
        compiler_params=_cparams(("arbitrary",)),
        name="merge_ln_route",
    )(x, uv, yb, sg, lng, lnb, ws, bs, wa, wb, wo, g1, b1, wrh, wrl, br)


def _expert_kernel(be_ref, bv_ref, x_ref, wg_ref, wu_ref, wd_ref, o_ref):
    i = pl.program_id(0)

    @pl.when(bv_ref[i] != 0)
    def _():
        x = x_ref[...]
        g = jnp.dot(x, wg_ref[...], preferred_element_type=F32)
        u = jnp.dot(x, wu_ref[...], preferred_element_type=F32)
        hm = (g * jax.nn.sigmoid(g) * u).astype(BF16)
        o_ref[...] = jnp.dot(hm, wd_ref[...], preferred_element_type=F32).astype(o_ref.dtype)

    @pl.when(bv_ref[i] == 0)
    def _():
        o_ref[...] = jnp.zeros(o_ref.shape, o_ref.dtype)


def _experts(xs, blk_e, blk_v, wg, wu, wd):
    rows, d = xs.shape
    nb = rows // MOE_ROWS
    grid_spec = pltpu.PrefetchScalarGridSpec(
        num_scalar_prefetch=2,
        grid=(nb,),
        in_specs=[pl.BlockSpec((MOE_ROWS, d), lambda i, be, bv: (i, 0)),
                  pl.BlockSpec((None, d, D_EXPERT), lambda i, be, bv: (be[i], 0, 0)),
                  pl.BlockSpec((None, d, D_EXPERT), lambda i, be, bv: (be[i], 0, 0)),
                  pl.BlockSpec((None, D_EXPERT, d), lambda i, be, bv: (be[i], 0, 0))],
        out_specs=pl.BlockSpec((MOE_ROWS, d), lambda i, be, bv: (i, 0)),
    )
    return pl.pallas_call(
        _expert_kernel,
        grid_spec=grid_spec,
        out_shape=jax.ShapeDtypeStruct((rows, d), BF16),
        compiler_params=_cparams(("arbitrary",)),
        name="expert_mlp",
    )(blk_e, blk_v, xs, wg, wu, wd)


def _final_kernel(h_ref, r1_ref, r2_ref, rt_ref, g_ref, b_ref, o_ref):
    w1 = rt_ref[:, 2:3]
    w2 = rt_ref[:, 3:4]
    f = r1_ref[...].astype(F32) * w1 + r2_ref[...].astype(F32) * w2
    o_ref[...] = _layer_norm(ALPHA * h_ref[...] + f, g_ref[...], b_ref[...])


def _final(h32, r1, r2, rt, g2, b2, tm):
    n, d = h32.shape
    row = lambda i: (i, 0)
    fixed = lambda i: (0, 0)
    return pl.pallas_call(
        _final_kernel,
        grid=(n // tm,),
        in_specs=[pl.BlockSpec((tm, d), row), pl.BlockSpec((tm, d), row), pl.BlockSpec((tm, d), row),
                  pl.BlockSpec((tm, LANES), row), pl.BlockSpec((1, d), fixed), pl.BlockSpec((1, d), fixed)],
        out_specs=pl.BlockSpec((tm, d), row),
        out_shape=jax.ShapeDtypeStruct((n, d), F32),
        compiler_params=_cparams(("arbitrary",)),
        name="combine_ln",
    )(h32, r1, r2, rt, g2, b2)


def _moe(h32, hb, rt, cnt, wg, wu, wd, g2, b2, tm):
    n, d = h32.shape
    a = n * TOP_K
    nb = -(-a // MOE_ROWS) + N_EXPERTS
    rows = nb * MOE_ROWS
    eidx = rt[:, 0:2].astype(jnp.int32)
    rank = rt[:, 4:6].astype(jnp.int32)
    counts = cnt[0, EXPERT_LANE0:EXPERT_LANE0 + N_EXPERTS].astype(jnp.int32)
    pcounts = (counts + MOE_ROWS - 1) // MOE_ROWS * MOE_ROWS
    pend = jnp.cumsum(pcounts)
    pstart = pend - pcounts
    dest = pstart[eidx] + rank
    blk_row = jnp.arange(nb, dtype=jnp.int32) * MOE_ROWS
    blk_e = jnp.minimum(jnp.searchsorted(pend, blk_row, side='right'), N_EXPERTS - 1).astype(jnp.int32)
    blk_v = (blk_row < pend[-1]).astype(jnp.int32)
    tok = jnp.broadcast_to(jnp.arange(n, dtype=jnp.int32)[:, None], (n, TOP_K))
    row_tok = jnp.full((rows,), n, jnp.int32).at[dest.reshape(-1)].set(tok.reshape(-1))
    hb_pad = jnp.concatenate([hb, jnp.zeros((1, d), hb.dtype)], axis=0)
    xs = hb_pad[row_tok]
    yb = _experts(xs, blk_e, blk_v, wg, wu, wd)
    r1 = yb[dest[:, 0]]
    r2 = yb[dest[:, 1]]
    return _final(h32, r1, r2, rt, g2, b2, tm)


def _stream(x2d, batch, prm, cache):
    n = x2d.shape[0]
    t = n // batch
    sample = cache is not None
    tm = 128 if sample else 1024
    uv = _proj_act(x2d, prm["w_uv"], prm["b_uv"], "gelu", tm, 1024)
    sg = _proj_act(x2d, prm["w_g"], prm["b_g"], "sigmoid", tm, 1024)
    qb, k32, kb, v32, vb, lf = _proj_qkv(x2d, prm["w_qkv"], prm["b_qkv"], prm["w_f"], prm["b_f"], min(tm, 512))
    logf = lf[:, :FOX_HEADS].reshape(batch, t, FOX_HEADS)
    lft = logf.transpose(0, 2, 1)
    q3 = qb.reshape(batch, t, FOX_W)
    k3 = kb.reshape(batch, t, FOX_W)
    v3 = vb.reshape(batch, t, FOX_W)
    if not sample:
        ct = _cumsum_lanes(lft)
        yb = _fox_prompt(q3, k3, v3, ct, 512)
    else:
        kc, vc, lfc = cache
        p = kc.shape[1]
        cc = _cumsum_lanes(lfc.transpose(0, 2, 1))
        pad = CUM_TILE - t
        cn = _cumsum_lanes(jnp.pad(lft, ((0, 0), (0, 0), (0, pad))))[:, :, :t] + cc[:, :, p - 1:p]
        yb = _fox_sample(q3, kc.reshape(batch, p, FOX_W), vc.reshape(batch, p, FOX_W), cc, k3, v3, cn, 1024)
    yb2 = yb.reshape(n, FOX_W)
    ws = prm["ws_s"] if sample else prm["ws_p"]
    bs = prm["bs_s"] if sample else prm["bs_p"]
    res = _merge(x2d, uv, yb2, sg, prm["lng"], prm["lnb"], ws, bs, prm["w_a"], prm["w_b"], prm["w_o"],
                 prm["g1"], prm["b1"], prm["wr_hi"], prm["wr_lo"], prm["b_r"], 128 if sample else 256, sample)
    h32, hb, rt, cnt = res[:4]
    y = _moe(h32, hb, rt, cnt, prm["w_gate"], prm["w_up"], prm["w_down"], prm["g2"], prm["b2"],
             128 if sample else 512)
    vn = res[4] if sample else None
    return y, k32, v32, logf, vn


def kernel(x_prompt, x_sample, cache_fox_k, cache_fox_v, cache_fox_logf, w_in, b_in, gmlp_ln_g, gmlp_ln_b,
           gmlp_ws, gmlp_bs, w_a, w_b, w_o, ln1_g, ln1_b, w_rg, b_rg, w_re, b_re, w_gate, w_up, w_down,
           ln2_g, ln2_b):
    assert w_in.shape[0] == DEPTH
    bp, s, d = x_prompt.shape
    bs_, t, _ = x_sample.shape
    l = 0
    wi = w_in[l]
    bi = b_in[l][None, :]
    wr = jnp.zeros((d, LANES), F32).at[:, :MOE_GROUPS].set(w_rg[l])
    wr = wr.at[:, EXPERT_LANE0:EXPERT_LANE0 + N_EXPERTS].set(w_re[l])
    br = jnp.zeros((1, LANES), F32).at[0, :MOE_GROUPS].set(b_rg[l])
    br = br.at[0, EXPERT_LANE0:EXPERT_LANE0 + N_EXPERTS].set(b_re[l])
    wr_hi = wr.astype(BF16)
    wr_lo = (wr - wr_hi.astype(F32)).astype(BF16)
    tril = jnp.tril(jnp.ones((GMLP_CHUNK, GMLP_CHUNK), bool))
    ws_p = jnp.where(tril[None], gmlp_ws[l], 0.0)
    bs_p = jnp.broadcast_to(gmlp_bs[l][:, :, None], (GMLP_GROUPS, GMLP_CHUNK, GMLP_DG))
    reps = GMLP_CHUNK // t
    eye = jnp.eye(reps, dtype=F32)
    ws_t = ws_p[:, :t, :t]
    ws_s = jnp.einsum('ab,gts->gatbs', eye, ws_t).reshape(GMLP_GROUPS, GMLP_CHUNK, GMLP_CHUNK)
    bs_s = jnp.broadcast_to(jnp.tile(gmlp_bs[l][:, :t], (1, reps))[:, :, None], (GMLP_GROUPS, GMLP_CHUNK, GMLP_DG))
    prm = {
        "w_uv": wi[:, OFF_U:OFF_Q].astype(BF16), "b_uv": bi[:, OFF_U:OFF_Q],
        "w_qkv": wi[:, OFF_Q:OFF_F].astype(BF16), "b_qkv": bi[:, OFF_Q:OFF_F],
        "w_f": jnp.pad(wi[:, OFF_F:OFF_GA], ((0, 0), (0, LANES - FOX_HEADS))).astype(BF16),
        "b_f": jnp.pad(bi[:, OFF_F:OFF_GA], ((0, 0), (0, LANES - FOX_HEADS))),
        "w_g": wi[:, OFF_GA:].astype(BF16), "b_g": bi[:, OFF_GA:],
        "lng": gmlp_ln_g[l][None, :], "lnb": gmlp_ln_b[l][None, :],
        "ws_p": ws_p.astype(BF16), "bs_p": bs_p, "ws_s": ws_s.astype(BF16), "bs_s": bs_s,
        "w_a": w_a[l].astype(BF16), "w_b": w_b[l].astype(BF16), "w_o": w_o[l].astype(BF16),
        "g1": ln1_g[l][None, :], "b1": ln1_b[l][None, :],
        "wr_hi": wr_hi, "wr_lo": wr_lo, "b_r": br,
        "w_gate": w_gate[l].astype(BF16), "w_up": w_up[l].astype(BF16), "w_down": w_down[l].astype(BF16),
        "g2": ln2_g[l][None, :], "b2": ln2_b[l][None, :],
    }
    yp, kp, vp, fp, _ = _stream(x_prompt.reshape(bp * s, d), bp, prm, None)
    cache = (cache_fox_k[l], cache_fox_v[l], cache_fox_logf[l])
    ys, ks, vs, fs, gv = _stream(x_sample.reshape(bs_ * t, d), bs_, prm, cache)
    hd = (FOX_HEADS, FOX_HD)
    return (yp.reshape(bp, s, d), ys.reshape(bs_, t, d),
            kp.reshape(1, bp, s, *hd), vp.reshape(1, bp, s, *hd), fp.reshape(1, bp, s, FOX_HEADS),
            ks.reshape(1, bs_, t, *hd), vs.reshape(1, bs_, t, *hd), fs.reshape(1, bs_, t, FOX_HEADS),
            gv.reshape(1, bs_, t, GMLP_GROUPS, GMLP_DG))
```
